```python
import functools
import jax, jax.numpy as jnp
from jax import lax
import numpy as np

D_MODEL = 2048
BATCH = 1
SEQ = 8192
DEPTH = 1
DEC_BATCH = 32
DEC_SEQ = 4
PAST_LEN = 16384
PAGE_SIZE = 128

SGU_WIDTH = 1024
SGU_GROUPS = 8
SGU_CH = SGU_WIDTH // SGU_GROUPS
CHUNK = 128
HEAD_DIM = 128
ROT_DIM = HEAD_DIM // 4
ROPE_THETA = 500000.0
DIL_CONFIGS = ((128, 1), (512, 4), (2048, 16))
N_DIL_GROUPS = 3
HEADS_PER_GROUP = 4
ATT_HEADS = N_DIL_GROUPS * HEADS_PER_GROUP
ATT_WIDTH = ATT_HEADS * HEAD_DIM
ATT_OUT = HEADS_PER_GROUP * HEAD_DIM
ATT_BLOCK = 128
PEER_HEADS = 8
N_KEYS = 128
N_EXPERTS = N_KEYS * N_KEYS
PEER_TOPK = 16
PEER_QDIM = 256
PEER_HALF = PEER_QDIM // 2
PEER_BLOCK = 128
IN_WIDTH = 2 * SGU_WIDTH + 3 * ATT_WIDTH + 2 * D_MODEL
EPS = 1e-6

kernel_name = 'hybrid_gmlp_dilated_attn_peer_decoder_step'


def rmsnorm(x, g):
    xf = x.astype(jnp.float32)
    y = xf * lax.rsqrt(jnp.mean(xf * xf, axis=-1, keepdims=True) + EPS)
    return y.astype(x.dtype) * g


def layernorm(x, g, b):
    xf = x.astype(jnp.float32)
    mu = jnp.mean(xf, axis=-1, keepdims=True)
    var = jnp.mean(jnp.square(xf - mu), axis=-1, keepdims=True)
    return ((xf - mu) * lax.rsqrt(var + EPS)).astype(x.dtype) * g + b


def modulate(h, shift, scale):
    return h * (1 + scale[:, None, :]) + shift[:, None, :]


def rope_partial(x, pos):
    inv = ROPE_THETA ** (-jnp.arange(0, ROT_DIM, 2, dtype=jnp.float32) / ROT_DIM)
    ang = pos.astype(jnp.float32)[:, None] * inv[None, :]
    cos = jnp.cos(ang)[None, :, None, :]
    sin = jnp.sin(ang)[None, :, None, :]
    xr = x[..., :ROT_DIM].astype(jnp.float32)
    x1, x2 = xr[..., :ROT_DIM // 2], xr[..., ROT_DIM // 2:]
    rot = jnp.concatenate([x1 * cos - x2 * sin, x2 * cos + x1 * sin], axis=-1)
    return jnp.concatenate([rot.astype(x.dtype), x[..., ROT_DIM:]], axis=-1)


def sgu_mix(u, v, w_s, b_s):
    B, T, _ = v.shape
    Tp = -(-T // CHUNK) * CHUNK
    vc = jnp.pad(v, ((0, 0), (0, Tp - T), (0, 0))).reshape(B, Tp // CHUNK, CHUNK, SGU_GROUPS, SGU_CH)
    causal = jnp.tril(jnp.ones((CHUNK, CHUNK), dtype=bool))
    ws = jnp.where(causal[None], w_s, 0)
    mixed = jnp.einsum('gij,bnjgc->bnigc', ws, vc) + b_s.T[None, None, :, :, None]
    return u * mixed.reshape(B, Tp, SGU_WIDTH)[:, :T]


def band_dilated_attention(q, k, v, dil, reach):
    B, S, H, E = q.shape
    span = dil * ATT_BLOCK
    Sp = -(-S // span) * span
    M = Sp // dil
    nb = M // ATT_BLOCK

    def to_sub(x):
        x = jnp.pad(x, ((0, 0), (0, Sp - S), (0, 0), (0, 0)))
        return x.reshape(B, M, dil, H, E).transpose(0, 2, 1, 3, 4)

    def key_blocks(x):
        xs = jnp.pad(to_sub(x), ((0, 0), (0, 0), (ATT_BLOCK, 0), (0, 0), (0, 0)))
        xs = xs.reshape(B, dil, nb + 1, ATT_BLOCK, H, E)
        return jnp.concatenate([xs[:, :, :-1], xs[:, :, 1:]], axis=3)

    qs = to_sub(q).reshape(B, dil, nb, ATT_BLOCK, H, E)
    ks, vs = key_blocks(k), key_blocks(v)
    s = jnp.einsum('bdnqhe,bdnkhe->bdnhqk', qs, ks).astype(jnp.float32) * (HEAD_DIM ** -0.5)
    qi = jnp.arange(ATT_BLOCK)[:, None]
    kj = jnp.arange(2 * ATT_BLOCK)[None, :]
    dist = ATT_BLOCK + qi - kj
    key_sub = (jnp.arange(nb)[:, None, None] - 1) * ATT_BLOCK + kj[None]
    mask = ((dist >= 0) & (dist <= reach))[None] & (key_sub >= 0)
    s = jnp.where(mask[None, None, :, None], s, -jnp.inf)
    lse = jax.nn.logsumexp(s, axis=-1)
    p = jnp.exp(s - lse[..., None])
    o = jnp.einsum('bdnhqk,bdnkhe->bdnqhe', p.astype(v.dtype), vs)
    o = o.reshape(B, dil, M, H, E).transpose(0, 2, 1, 3, 4).reshape(B, Sp, H, E)[:, :S]
    lse = lse.transpose(0, 1, 2, 4, 3).reshape(B, dil, M, H).transpose(0, 2, 1, 3).reshape(B, Sp, H)[:, :S]
    return o, lse


def window_dilated_attention(q, kv_all, L, dil, reach):
    T = q.shape[1]
    row = L + jnp.arange(T)[:, None] - dil * jnp.arange(reach + 1)[None, :]
    valid = row >= 0
    kv = kv_all[:, jnp.maximum(row, 0)]
    s = jnp.einsum('bthe,btjhe->bthj', q, kv[:, :, :, 0]).astype(jnp.float32) * (HEAD_DIM ** -0.5)
    s = jnp.where(valid[None, :, None, :], s, -jnp.inf)
    lse = jax.nn.logsumexp(s, axis=-1)
    p = jnp.exp(s - lse[..., None])
    o = jnp.einsum('bthj,btjhe->bthe', p.astype(q.dtype), kv[:, :, :, 1])
    return o, lse


def combine_groups(outs, lses):
    w = jax.nn.softmax(jnp.stack(lses, axis=0), axis=0)
    o = jnp.sum(w[..., None] * jnp.stack(outs, axis=0).astype(jnp.float32), axis=0)
    return o.astype(outs[0].dtype)


def attend_prompt(q, k, v):
    T = q.shape[1]
    outs, lses, states = [], [], []
    for g, (win, dil) in enumerate(DIL_CONFIGS):
        o, l = band_dilated_attention(q[:, :, g], k[:, :, g], v[:, :, g], dil, win // dil)
        outs.append(o)
        lses.append(l)
        keep = min(win, T)
        states.append(jnp.stack([k[:, T - keep:, g], v[:, T - keep:, g]], axis=2))
    return combine_groups(outs, lses), states


def attend_sample(q, k, v, buffers):
    T = q.shape[1]
    outs, lses, states = [], [], []
    for g, (win, dil) in enumerate(DIL_CONFIGS):
        buf = buffers[g]
        L = buf.shape[1]
        kv_all = jnp.concatenate([buf, jnp.stack([k[:, :, g], v[:, :, g]], axis=2)], axis=1)
        o, l = window_dilated_attention(q[:, :, g], kv_all, L, dil, win // dil)
        outs.append(o)
        lses.append(l)
        keep = min(win, L + T)
        states.append(kv_all[:, L + T - keep:])
    return combine_groups(outs, lses), states


def token_mixer(h, pos, attend, w_in, sgu_ln_g, sgu_ln_b, w_s, b_s, w_out_a, w_out_b, w_out):
    B, T, _ = h.shape
    sizes = (SGU_WIDTH, SGU_WIDTH, ATT_WIDTH, ATT_WIDTH, ATT_WIDTH, D_MODEL)
    cuts = [int(c) for c in np.cumsum(sizes)]
    u_a, v_a, q, k, v, g_a, g_b = jnp.split(h @ w_in, cuts, axis=-1)
    u_a = jax.nn.gelu(u_a)
    v_a = layernorm(jax.nn.gelu(v_a), sgu_ln_g, sgu_ln_b)
    y_a = sgu_mix(u_a, v_a, w_s, b_s)
    shape5 = (B, T, N_DIL_GROUPS, HEADS_PER_GROUP, HEAD_DIM)
    q = rope_partial(q.reshape(B, T, ATT_HEADS, HEAD_DIM), pos).reshape(shape5)
    k = rope_partial(k.reshape(B, T, ATT_HEADS, HEAD_DIM), pos).reshape(shape5)
    v = v.reshape(shape5)
    o_att, win_states = attend(q, k, v)
    y_b = o_att.reshape(B, T, ATT_OUT)
    merged = jax.nn.sigmoid(g_a) * (y_a @ w_out_a) + jax.nn.sigmoid(g_b) * (y_b @ w_out_b)
    return merged @ w_out, win_states, v_a


def peer_ffn(x, w_query, sub_keys, expert_u, expert_v):
    N, D = x.shape
    Np = -(-N // PEER_BLOCK) * PEER_BLOCK
    xb = jnp.pad(x, ((0, Np - N), (0, 0))).reshape(Np // PEER_BLOCK, PEER_BLOCK, D)

    def block(xt):
        q = (xt @ w_query).reshape(PEER_BLOCK, PEER_HEADS, 2, PEER_HALF)
        s = jnp.einsum('thce,hcne->thcn', q, sub_keys).astype(jnp.float32)
        sv, si = lax.top_k(s, PEER_TOPK)
        cand = (sv[:, :, 0, :, None] + sv[:, :, 1, None, :]).reshape(PEER_BLOCK, PEER_HEADS, PEER_TOPK * PEER_TOPK)
        fv, fi = lax.top_k(cand, PEER_TOPK)
        i1 = jnp.take_along_axis(si[:, :, 0], fi // PEER_TOPK, axis=-1)
        i2 = jnp.take_along_axis(si[:, :, 1], fi % PEER_TOPK, axis=-1)
        eid = i1 * N_KEYS + i2
        gate = jax.nn.softmax(fv, axis=-1)
        act = jax.nn.gelu(jnp.einsum('td,thkd->thk', xt, expert_u[eid]))
        return jnp.einsum('thk,thkd->td', (gate * act).astype(xt.dtype), expert_v[eid])

    return lax.map(block, xb).reshape(Np, D)[:N]


def block_forward(x, c, pos, attend, norm1_g, norm2_g, w_ada, b_ada, w_in, sgu_ln_g, sgu_ln_b,
                  w_s, b_s, w_out_a, w_out_b, w_out, w_query, sub_keys, expert_u, expert_v):
    ada = jax.nn.silu(c) @ w_ada + b_ada
    sh1, sc1, g1, sh2, sc2, g2 = jnp.split(ada, 6, axis=-1)
    h = modulate(rmsnorm(x, norm1_g), sh1, sc1)
    mix, win_states, v_rows = token_mixer(h, pos, attend, w_in, sgu_ln_g, sgu_ln_b, w_s, b_s,
                                          w_out_a, w_out_b, w_out)
    x = x + g1[:, None, :] * mix
    h = modulate(rmsnorm(x, norm2_g), sh2, sc2)
    B, T, D = h.shape
    x = x + g2[:, None, :] * peer_ffn(h.reshape(B * T, D), w_query, sub_keys, expert_u, expert_v).reshape(B, T, D)
    return x, win_states, v_rows


def setup_inputs(seed: int = 0) -> dict:
    key = jax.random.key(seed)
    ks = jax.random.split(key, 32)
    n = lambda i, shape: jax.random.normal(ks[i], shape, dtype=jnp.float32)
    D = D_MODEL
    wins = [min(w, PAST_LEN) for (w, _) in DIL_CONFIGS]
    st = lambda i, L: n(i, (DEPTH, DEC_BATCH, L, 2, HEADS_PER_GROUP, HEAD_DIM))
    return {
        'x_prompt': n(0, (BATCH, SEQ, D)),
        'x_sample': n(1, (DEC_BATCH, DEC_SEQ, D)),
        'state_win1': st(2, wins[0]),
        'state_win2': st(3, wins[1]),
        'state_win3': st(4, wins[2]),
        'c_prompt': n(5, (BATCH, D)),
        'c_sample': n(6, (DEC_BATCH, D)),
        'norm1_g': 1.0 + 0.02 * n(7, (DEPTH, D)),
        'norm2_g': 1.0 + 0.02 * n(8, (DEPTH, D)),
        'w_ada': n(9, (DEPTH, D, 6 * D)) * (0.5 * D ** -0.5),
        'b_ada': 0.02 * n(10, (DEPTH, 6 * D)),
        'w_in': n(11, (DEPTH, D, IN_WIDTH)) * D ** -0.5,
        'sgu_ln_g': 1.0 + 0.02 * n(12, (DEPTH, SGU_WIDTH)),
        'sgu_ln_b': 0.02 * n(13, (DEPTH, SGU_WIDTH)),
        'w_s': n(14, (DEPTH, SGU_GROUPS, CHUNK, CHUNK)) * CHUNK ** -0.5,
        'b_s': 1.0 + 0.02 * n(15, (DEPTH, SGU_GROUPS, CHUNK)),
        'w_out_a': n(16, (DEPTH, SGU_WIDTH, D)) * SGU_WIDTH ** -0.5,
        'w_out_b': n(17, (DEPTH, ATT_OUT, D)) * ATT_OUT ** -0.5,
        'w_out': n(18, (DEPTH, D, D)) * D ** -0.5,
        'w_query': n(19, (DEPTH, D, PEER_HEADS * PEER_QDIM)) * D ** -0.5,
        'sub_keys': n(20, (DEPTH, PEER_HEADS, 2, N_KEYS, PEER_HALF)) * PEER_HALF ** -0.5,
        'expert_u': n(21, (DEPTH, N_EXPERTS, D)) * D ** -0.5,
        'expert_v': n(22, (DEPTH, N_EXPERTS, D)) * 0.5,
        'final_g': 1.0 + 0.02 * n(23, (D,)),
    }


def reference(x_prompt, x_sample, state_win1, state_win2, state_win3, c_prompt, c_sample,
              norm1_g, norm2_g, w_ada, b_ada, w_in, sgu_ln_g, sgu_ln_b, w_s, b_s,
              w_out_a, w_out_b, w_out, w_query, sub_keys, expert_u, expert_v, final_g):
    pos_p = jnp.arange(x_prompt.shape[1], dtype=jnp.int32)
    pos_s = PAST_LEN + jnp.arange(x_sample.shape[1], dtype=jnp.int32)
    xp, xs = x_prompt, x_sample
    new_p = [[], [], []]
    new_s = [[], [], []]
    new_v = []
    for l in range(DEPTH):
        wl = (norm1_g[l], norm2_g[l], w_ada[l], b_ada[l], w_in[l], sgu_ln_g[l], sgu_ln_b[l],
              w_s[l], b_s[l], w_out_a[l], w_out_b[l], w_out[l], w_query[l], sub_keys[l],
              expert_u[l], expert_v[l])
        xp, win_p, _ = block_forward(xp, c_prompt, pos_p, attend_prompt, *wl)
        attend_s = functools.partial(attend_sample, buffers=(state_win1[l], state_win2[l], state_win3[l]))
        xs, win_s, v_rows = block_forward(xs, c_sample, pos_s, attend_s, *wl)
        for g in range(N_DIL_GROUPS):
            new_p[g].append(win_p[g])
            new_s[g].append(win_s[g])
        new_v.append(v_rows)
    y_prompt = rmsnorm(xp, final_g)
    y_sample = rmsnorm(xs, final_g)
    return (y_prompt, y_sample,
            jnp.stack(new_p[0]), jnp.stack(new_p[1]), jnp.stack(new_p[2]),
            jnp.stack(new_s[0]), jnp.stack(new_s[1]), jnp.stack(new_s[2]),
            jnp.stack(new_v))
```

```python
import functools

import jax
import jax.numpy as jnp
from jax import lax
from jax.experimental import pallas as pl
from jax.experimental.pallas import tpu as pltpu

F32 = jnp.float32
BF16 = jnp.bfloat16

PAST_LEN = 16384
SGU_WIDTH = 1024
SGU_GROUPS = 8
SGU_CH = SGU_WIDTH // SGU_GROUPS
CHUNK = 128
HEAD_DIM = 128
ROT_DIM = HEAD_DIM // 4
ROPE_THETA = 500000.0
DIL_CONFIGS = ((128, 1), (512, 4), (2048, 16))
N_DIL_GROUPS = 3
HEADS_PER_GROUP = 4
ATT_HEADS = N_DIL_GROUPS * HEADS_PER_GROUP
ATT_WIDTH = ATT_HEADS * HEAD_DIM
ATT_OUT = HEADS_PER_GROUP * HEAD_DIM
ATT_BLOCK = 128
PEER_HEADS = 8
N_KEYS = 128
PEER_TOPK = 16
PEER_HALF = 128
EPS = 1e-6

LANES = 128
VMEM_LIMIT_BYTES = 56 * 1024 * 1024
NEG_BIG = -1e30
NOT_RANKED = 64.0


def _cparams(*sem):
    return pltpu.CompilerParams(dimension_semantics=sem, vmem_limit_bytes=VMEM_LIMIT_BYTES)


def _dot(a, b):
    return jnp.dot(a, b, preferred_element_type=F32)


def _dot_nt(a, b):
    return lax.dot_general(a, b, (((1,), (1,)), ((), ())), preferred_element_type=F32)


def _dot_tn(a, b):
    return lax.dot_general(a, b, (((0,), (0,)), ((), ())), preferred_element_type=F32)


def _row_tile(m, cap):
    t = min(m, cap)
    while m % t:
        t //= 2
    return t


def _normmod(x, g, sc, sh):
    y = x * lax.rsqrt(jnp.mean(x * x, axis=-1, keepdims=True) + EPS)
    return (y * g) * (1.0 + sc) + sh


def _mod_spec(ms, tm, d):
    if ms == 1:
        return pl.BlockSpec((1, d), lambda i, j: (0, 0))
    return pl.BlockSpec((tm, d), lambda i, j: (i, 0))


def _ada_body(c_ref, w_ref, b_ref, o_ref):
    c = c_ref[...]
    a = (c * jax.nn.sigmoid(c)).astype(BF16)
    o_ref[...] = _dot(a, w_ref[...]) + b_ref[...]


def _ada_call(c_all, w_ada, b_ada):
    mc, d = c_all.shape
    n = w_ada.shape[1]
    tn = 1024
    return pl.pallas_call(
        _ada_body,
        out_shape=jax.ShapeDtypeStruct((mc, n), F32),
        grid=(n // tn,),
        in_specs=[
            pl.BlockSpec((mc, d), lambda j: (0, 0)),
            pl.BlockSpec((d, tn), lambda j: (0, j)),
            pl.BlockSpec((1, tn), lambda j: (0, j)),
        ],
        out_specs=pl.BlockSpec((mc, tn), lambda j: (0, j)),
        compiler_params=_cparams("arbitrary"),
        name="ada_proj",
    )(c_all, w_ada, b_ada)


def _ep_none(acc):
    return acc


def _ep_gelu(acc):
    return jax.nn.gelu(acc)


def _ep_sigmoid(acc):
    return jax.nn.sigmoid(acc)


def _ep_gelu_ln(acc, g, b):
    y = jax.nn.gelu(acc)
    mu = jnp.mean(y, axis=-1, keepdims=True)
    var = jnp.mean(jnp.square(y - mu), axis=-1, keepdims=True)
    return ((y - mu) * lax.rsqrt(var + EPS)) * g + b


def _ep_rope(acc, c, s_lo, s_hi):
    outs = []
    for hh in range(acc.shape[1] // LANES):
        a = acc[:, hh * LANES:(hh + 1) * LANES]
        outs.append(a * c + pltpu.roll(a, LANES - ROT_DIM // 2, 1) * s_lo + pltpu.roll(a, ROT_DIM // 2, 1) * s_hi)
    return jnp.concatenate(outs, axis=1)


def _proj_body(*refs, epilogue, n_extra, want_h, split_heads):
    x_ref, g_ref, sc_ref, sh_ref, w_ref = refs[:5]
    extra = refs[5:5 + n_extra]
    out_ref = refs[5 + n_extra]
    h_scr = refs[-1]

    @pl.when(pl.program_id(1) == 0)
    def _():
        h = _normmod(x_ref[...], g_ref[...], sc_ref[...], sh_ref[...]).astype(BF16)
        h_scr[...] = h
        if want_h:
            refs[6 + n_extra][...] = h

    out = epilogue(_dot(h_scr[...], w_ref[...]), *[r[...] for r in extra])
    if split_heads:
        for k in range(out.shape[1] // LANES):
            out_ref[k] = out[:, k * LANES:(k + 1) * LANES]
    else:
        out_ref[...] = out.astype(out_ref.dtype)


def _proj_call(x, g, sc, sh, w, col0, ncols, tn, epilogue, *, row_extras=(), col_extras=(), want_h=False,
               split_heads=False, name="proj"):
    m, d = x.shape
    tm = _row_tile(m, 1024)
    assert col0 % tn == 0 and ncols % tn == 0
    jb = col0 // tn
    in_specs = [
        pl.BlockSpec((tm, d), lambda i, j: (i, 0)),
        pl.BlockSpec((1, d), lambda i, j: (0, 0)),
        _mod_spec(sc.shape[0], tm, d),
        _mod_spec(sh.shape[0], tm, d),
        pl.BlockSpec((d, tn), lambda i, j: (0, j + jb)),
    ]
    for e in row_extras:
        in_specs.append(pl.BlockSpec((tm, e.shape[1]), lambda i, j: (i, 0)))
    for e in col_extras:
        in_specs.append(pl.BlockSpec((1, tn), lambda i, j: (0, j)))
    if split_heads:
        out_shape = [jax.ShapeDtypeStruct((ncols // LANES, m, LANES), F32)]
        out_specs = [pl.BlockSpec((tn // LANES, tm, LANES), lambda i, j: (j, i, 0))]
    else:
        out_shape = [jax.ShapeDtypeStruct((m, ncols), F32)]
        out_specs = [pl.BlockSpec((tm, tn), lambda i, j: (i, j))]
    if want_h:
        out_shape.append(jax.ShapeDtypeStruct((m, d), BF16))
        out_specs.append(pl.BlockSpec((tm, d), lambda i, j: (i, 0)))
    res = pl.pallas_call(
        functools.partial(_proj_body, epilogue=epilogue, n_extra=len(row_extras) + len(col_extras),
                          want_h=want_h, split_heads=split_heads),
        out_shape=out_shape,
        grid=(m // tm, ncols // tn),
        in_specs=in_specs,
        out_specs=out_specs,
        scratch_shapes=[pltpu.VMEM((tm, d), BF16)],
        compiler_params=_cparams("parallel", "arbitrary"),
        name=name,
    )(x, g, sc, sh, w, *row_extras, *col_extras)
    return res if want_h else res[0]


def _sgu_body(u_ref, v_ref, ws_ref, bs_ref, o_ref):
    rows = u_ref.shape[0]
    causal = lax.broadcasted_iota(jnp.int32, (CHUNK, CHUNK), 0) >= lax.broadcasted_iota(jnp.int32, (CHUNK, CHUNK), 1)
    for g in range(SGU_GROUPS):
        ws = jnp.where(causal, ws_ref[g], 0.0).astype(BF16)
        cols = slice(g * LANES, (g + 1) * LANES)
        for c in range(rows // CHUNK):
            rs = slice(c * CHUNK, (c + 1) * CHUNK)
            mixed = _dot(ws, v_ref[rs, cols].astype(BF16)) + bs_ref[:, cols]
            o_ref[rs, cols] = (u_ref[rs, cols] * mixed).astype(o_ref.dtype)


def _sgu_call(u, v, w_s, bs_rows):
    m, w = u.shape
    rb = _row_tile(m, 512)
    return pl.pallas_call(
        _sgu_body,
        out_shape=jax.ShapeDtypeStruct((m, w), BF16),
        grid=(m // rb,),
        in_specs=[
            pl.BlockSpec((rb, w), lambda i: (i, 0)),
            pl.BlockSpec((rb, w), lambda i: (i, 0)),
            pl.BlockSpec(w_s.shape, lambda i: (0, 0, 0)),
            pl.BlockSpec(bs_rows.shape, lambda i: (0, 0)),
        ],
        out_specs=pl.BlockSpec((rb, w), lambda i: (i, 0)),
        compiler_params=_cparams("parallel"),
        name="sgu_mix",
    )(u, v, w_s, bs_rows)


def _sgu_new_chunk_body(u_ref, v_ref, wl_ref, bl_ref, o_ref):
    t_new = u_ref.shape[0]
    for i in range(t_new):
        mixed = bl_ref[i:i + 1, :]
        for j in range(i + 1):
            mixed = mixed + wl_ref[i, j:j + 1, :] * v_ref[j]
        o_ref[i] = (u_ref[i] * mixed).astype(o_ref.dtype)


def _sgu_new_chunk_call(u, v, w_lanes, b_lanes):
    return pl.pallas_call(
        _sgu_new_chunk_body,
        out_shape=jax.ShapeDtypeStruct(u.shape, BF16),
        name="sgu_mix_new_chunk",
    )(u, v, w_lanes, b_lanes)


def _band_attn_body(q_ref, kc_ref, kp_ref, vc_ref, vp_ref, o_ref, lse_ref, kx, vx, *, dil, reach, rb):
    span = dil * ATT_BLOCK
    n = pl.program_id(0)
    kx[0:span, :] = kp_ref[...]
    kx[span:span + rb, :] = kc_ref[...]
    vx[0:span, :] = vp_ref[...]
    vx[span:span + rb, :] = vc_ref[...]
    qi = lax.broadcasted_iota(jnp.int32, (ATT_BLOCK, ATT_BLOCK), 0)
    kj = lax.broadcasted_iota(jnp.int32, (ATT_BLOCK, ATT_BLOCK), 1)
    dist_c = qi - kj
    dist_p = ATT_BLOCK + qi - kj
    mask_c = (dist_c >= 0) & (dist_c <= reach)
    mask_p0 = (dist_p >= 0) & (dist_p <= reach)
    scale = HEAD_DIM ** -0.5

    def rows(start):
        if dil == 1:
            return pl.ds(start, ATT_BLOCK)
        return pl.ds(start, ATT_BLOCK, stride=dil)

    def step(it, carry):
        a = it // dil
        r = it % dil
        base = a * span + r
        q = q_ref[rows(base), :].astype(BF16)
        kc = kx[rows(span + base), :].astype(BF16)
        kp = kx[rows(base), :].astype(BF16)
        vc = vx[rows(span + base), :].astype(BF16)
        vp = vx[rows(base), :].astype(BF16)
        has_prev = (n * rb + a * span) > 0
        s_c = jnp.where(mask_c, _dot_nt(q, kc) * scale, NEG_BIG)
        s_p = jnp.where(mask_p0, _dot_nt(q, kp) * scale, NEG_BIG)
        s_p = jnp.where(has_prev, s_p, NEG_BIG)
        m = jnp.maximum(jnp.max(s_c, axis=1, keepdims=True), jnp.max(s_p, axis=1, keepdims=True))
        p_c = jnp.exp(s_c - m)
        p_p = jnp.exp(s_p - m)
        l = jnp.sum(p_c, axis=1, keepdims=True) + jnp.sum(p_p, axis=1, keepdims=True)
        o = (_dot(p_c.astype(BF16), vc) + _dot(p_p.astype(BF16), vp)) / l
        o_ref[rows(base), :] = o
        lse_ref[rows(base), :] = jnp.broadcast_to(m + jnp.log(l), (ATT_BLOCK, HEAD_DIM))
        return carry

    lax.fori_loop(0, rb // ATT_BLOCK, step, 0)


def _band_attn_call(qk, v, group, dil, reach):
    t = qk.shape[0]
    span = dil * ATT_BLOCK
    rb = max(span, _row_tile(t, 2048))
    assert t % rb == 0 and rb % span == 0
    per = rb // span
    qcol = group * HEADS_PER_GROUP
    kcol = ATT_HEADS + group * HEADS_PER_GROUP
    cur = lambda c0: pl.BlockSpec((rb, HEAD_DIM), lambda n, h: (n, c0 + h))
    prev = lambda c0: pl.BlockSpec((span, HEAD_DIM), lambda n, h: (jnp.maximum(n * per - 1, 0), c0 + h))
    out_spec = pl.BlockSpec((rb, HEAD_DIM), lambda n, h: (n, h))
    return pl.pallas_call(
        functools.partial(_band_attn_body, dil=dil, reach=reach, rb=rb),
        out_shape=[jax.ShapeDtypeStruct((t, ATT_OUT), F32)] * 2,
        grid=(t // rb, HEADS_PER_GROUP),
        in_specs=[cur(qcol), cur(kcol), prev(kcol), cur(qcol), prev(qcol)],
        out_specs=[out_spec, out_spec],
        scratch_shapes=[pltpu.VMEM((span + rb, HEAD_DIM), F32)] * 2,
        compiler_params=_cparams("parallel", "parallel"),
        name=f"band_attn_d{dil}",
    )(qk, qk, qk, v, v)


def _win_attn_body(q_ref, new_ref, st_ref, o_ref, lse_ref, win_ref, *, dil, reach, t_new, win_len):
    rows_per_pos = 2 * HEADS_PER_GROUP
    shift = t_new * rows_per_pos
    total = win_len * rows_per_pos
    win_ref[0, 0:total - shift, :] = st_ref[0, shift:total, :]
    win_ref[0, total - shift:total, :] = new_ref[0]
    scale = HEAD_DIM ** -0.5
    m_idx = lax.broadcasted_iota(jnp.int32, (ATT_BLOCK, 1), 0)
    for t in range(t_new):
        res = t % dil
        first = t // dil
        for h in range(HEADS_PER_GROUP):
            q = q_ref[0, t:t + 1, h * HEAD_DIM:(h + 1) * HEAD_DIM]
            kb = st_ref[0, pl.ds(res * rows_per_pos + h, ATT_BLOCK, stride=rows_per_pos * dil), :]
            vb = st_ref[0, pl.ds(res * rows_per_pos + HEADS_PER_GROUP + h, ATT_BLOCK, stride=rows_per_pos * dil), :]
            s_b = jnp.sum(kb * q, axis=1, keepdims=True) * scale
            s_b = jnp.where(m_idx >= first, s_b, NEG_BIG)
            new_keys = [tp for tp in range(t + 1) if (t - tp) % dil == 0 and (t - tp) // dil <= reach]
            s_n = []
            for tp in new_keys:
                kn = new_ref[0, tp * rows_per_pos + h:tp * rows_per_pos + h + 1, :]
                s_n.append(jnp.sum(kn * q, axis=1, keepdims=True) * scale)
            mx = jnp.max(s_b, axis=0, keepdims=True)
            for s in s_n:
                mx = jnp.maximum(mx, s)
            p_b = jnp.exp(s_b - mx)
            l = jnp.sum(p_b, axis=0, keepdims=True)
            acc = jnp.sum(p_b * vb, axis=0, keepdims=True)
            for tp, s in zip(new_keys, s_n):
                p = jnp.exp(s - mx)
                l = l + p
                r0 = tp * rows_per_pos + HEADS_PER_GROUP + h
                acc = acc + p * new_ref[0, r0:r0 + 1, :]
            o_ref[0, t:t + 1, h * HEAD_DIM:(h + 1) * HEAD_DIM] = acc / l
            lse_ref[0, t:t + 1, h * HEAD_DIM:(h + 1) * HEAD_DIM] = jnp.broadcast_to(mx + jnp.log(l), (1, HEAD_DIM))


def _win_attn_call(q_g, new_rows, state, dil, reach):
    b, t_new, _ = q_g.shape
    flat = state.shape[1]
    win_len = flat // (2 * HEADS_PER_GROUP)
    assert win_len == reach * dil and reach == ATT_BLOCK and t_new * 2 * HEADS_PER_GROUP % 8 == 0
    qspec = pl.BlockSpec((1, t_new, ATT_OUT), lambda i: (i, 0, 0))
    return pl.pallas_call(
        functools.partial(_win_attn_body, dil=dil, reach=reach, t_new=t_new, win_len=win_len),
        out_shape=[jax.ShapeDtypeStruct(q_g.shape, F32), jax.ShapeDtypeStruct(q_g.shape, F32),
                   jax.ShapeDtypeStruct(state.shape, F32)],
        grid=(b,),
        in_specs=[qspec,
                  pl.BlockSpec((1,) + new_rows.shape[1:], lambda i: (i, 0, 0)),
                  pl.BlockSpec((1, flat, HEAD_DIM), lambda i: (i, 0, 0))],
        out_specs=[qspec, qspec, pl.BlockSpec((1, flat, HEAD_DIM), lambda i: (i, 0, 0))],
        compiler_params=_cparams("parallel"),
        name=f"win_attn_d{dil}",
    )(q_g, new_rows, state)


def _merge_body(o1, o2, o3, l1, l2, l3, y_ref):
    a, b, c = l1[...], l2[...], l3[...]
    mx = jnp.maximum(jnp.maximum(a, b), c)
    ea, eb, ec = jnp.exp(a - mx), jnp.exp(b - mx), jnp.exp(c - mx)
    y = (ea * o1[...] + eb * o2[...] + ec * o3[...]) / (ea + eb + ec)
    y_ref[...] = y.astype(y_ref.dtype)


def _merge_call(outs, lses):
    m, w = outs[0].shape
    tm = _row_tile(m, 1024)
    spec = pl.BlockSpec((tm, w), lambda i: (i, 0))
    return pl.pallas_call(
        _merge_body,
        out_shape=jax.ShapeDtypeStruct((m, w), BF16),
        grid=(m // tm,),
        in_specs=[spec] * 6,
        out_specs=spec,
        compiler_params=_cparams("parallel"),
        name="merge_groups",
    )(*outs, *lses)


def _branch_body(ya_ref, yb_ref, wa_ref, wb_ref, ga_ref, gb_ref, o_ref):
    o = ga_ref[...] * _dot(ya_ref[...], wa_ref[...]) + gb_ref[...] * _dot(yb_ref[...], wb_ref[...])
    o_ref[...] = o.astype(o_ref.dtype)


def _branch_call(ya, yb, w_out_a, w_out_b, gates):
    m = ya.shape[0]
    d = w_out_a.shape[1]
    tm = _row_tile(m, 1024)
    tn = 512
    nj = d // tn
    return pl.pallas_call(
        _branch_body,
        out_shape=jax.ShapeDtypeStruct((m, d), BF16),
        grid=(m // tm, nj),
        in_specs=[
            pl.BlockSpec((tm, ya.shape[1]), lambda i, j: (i, 0)),
            pl.BlockSpec((tm, yb.shape[1]), lambda i, j: (i, 0)),
            pl.BlockSpec((ya.shape[1], tn), lambda i, j: (0, j)),
            pl.BlockSpec((yb.shape[1], tn), lambda i, j: (0, j)),
            pl.BlockSpec((tm, tn), lambda i, j: (i, j)),
            pl.BlockSpec((tm, tn), lambda i, j: (i, j + nj)),
        ],
        out_specs=pl.BlockSpec((tm, tn), lambda i, j: (i, j)),
        compiler_params=_cparams("parallel", "arbitrary"),
        name="branch_proj",
    )(ya, yb, w_out_a, w_out_b, gates, gates)


def _outproj_body(mg_ref, w_ref, x_ref, g_ref, o_ref):
    o_ref[...] = x_ref[...] + g_ref[...] * _dot(mg_ref[...], w_ref[...])


def _outproj_call(merged, w_out, x, g1):
    m, d = x.shape
    tm = _row_tile(m, 1024)
    tn = 512
    if g1.shape[0] == 1:
        gspec = pl.BlockSpec((1, tn), lambda i, j: (0, j))
    else:
        gspec = pl.BlockSpec((tm, tn), lambda i, j: (i, j))
    return pl.pallas_call(
        _outproj_body,
        out_shape=jax.ShapeDtypeStruct((m, d), F32),
        grid=(m // tm, d // tn),
        in_specs=[
            pl.BlockSpec((tm, d), lambda i, j: (i, 0)),
            pl.BlockSpec((d, tn), lambda i, j: (0, j)),
            pl.BlockSpec((tm, tn), lambda i, j: (i, j)),
            gspec,
        ],
        out_specs=pl.BlockSpec((tm, tn), lambda i, j: (i, j)),
        compiler_params=_cparams("parallel", "arbitrary"),
        name="out_proj",
    )(merged, w_out, x, g1)


def _take_topk(s, order):
    vals, hots = [], []
    for _ in range(PEER_TOPK):
        m = jnp.max(s, axis=0, keepdims=True)
        first = jnp.min(jnp.where(s == m, order, float(1 << 20)), axis=0, keepdims=True)
        hot = order == first
        s = jnp.where(hot, -jnp.inf, s)
        vals.append(m)
        hots.append(hot)
    return vals, hots


def _select_body(q_ref, sk_ref, n0_ref, e0_ref, r1_ref, e1_ref):
    tq = q_ref.shape[1]
    key_idx = lax.broadcasted_iota(jnp.int32, (N_KEYS, tq), 0).astype(F32)
    scores, tops, ranks = [], [], []
    for c in range(2):
        s = _dot_nt(sk_ref[0, c], q_ref[c].astype(BF16))
        vals, hots = _take_topk(s, key_idx)
        rank = jnp.full((N_KEYS, tq), NOT_RANKED, F32)
        for r, hot in enumerate(hots):
            rank = jnp.where(hot, float(r), rank)
        scores.append(s)
        tops.append(vals)
        ranks.append(rank)

    top1 = jnp.concatenate(tops[1], axis=0)
    parts = [tops[0][0] + top1] + [tops[0][a] + top1[0:8] for a in range(1, PEER_TOPK)]
    cand = jnp.concatenate(parts, axis=0)
    n_rows = cand.shape[0]
    row = lax.broadcasted_iota(jnp.int32, (n_rows, tq), 0)
    a_idx = jnp.where(row < PEER_TOPK, 0, ((row - PEER_TOPK) >> 3) + 1)
    b_idx = jnp.where(row < PEER_TOPK, row, (row - PEER_TOPK) & 7)
    cand = jnp.where((a_idx + 1) * (b_idx + 1) <= PEER_TOPK, cand, -jnp.inf)
    vals, hots = _take_topk(cand, (a_idx * PEER_TOPK + b_idx).astype(F32))
    z = jnp.zeros_like(vals[0])
    taken = jnp.zeros((n_rows, tq), F32)
    for m, hot in zip(vals, hots):
        z = z + jnp.exp(m - vals[0])
        taken = jnp.where(hot, 1.0, taken)
    counts = [jnp.sum(taken[0:PEER_TOPK], axis=0, keepdims=True)]
    for a in range(1, PEER_TOPK):
        lo = PEER_TOPK + 8 * (a - 1)
        counts.append(jnp.sum(taken[lo:lo + 8], axis=0, keepdims=True))
    n0 = jnp.zeros((N_KEYS, tq), F32)
    for a in range(PEER_TOPK):
        n0 = n0 + jnp.where(ranks[0] == float(a), counts[a], 0.0)
    n0_ref[0] = n0
    e0_ref[0] = jnp.exp(scores[0] - tops[0][0]) / z
    r1_ref[0] = ranks[1]
    e1_ref[0] = jnp.exp(scores[1] - tops[1][0])


def _select_call(q_heads, sub_keys):
    _, m, _ = q_heads.shape
    tq = _row_tile(m, 256)
    tspec = pl.BlockSpec((1, N_KEYS, tq), lambda i, h: (h, 0, i))
    return pl.pallas_call(
        _select_body,
        out_shape=[jax.ShapeDtypeStruct((PEER_HEADS, N_KEYS, m), F32)] * 4,
        grid=(m // tq, PEER_HEADS),
        in_specs=[
            pl.BlockSpec((2, tq, PEER_HALF), lambda i, h: (h, i, 0)),
            pl.BlockSpec((1, 2, N_KEYS, PEER_HALF), lambda i, h: (h, 0, 0, 0)),
        ],
        out_specs=[tspec] * 4,
        compiler_params=_cparams("parallel", "parallel"),
        name="peer_select",
    )(q_heads, sub_keys)


def _peer_body(h_ref, u_ref, v_ref, n0_ref, e0_ref, r1_ref, e1_ref, x_ref, g2_ref, fg_ref, o_ref,
               acc, st, gs, *, te, tm, final_norm):
    j = pl.program_id(1)

    @pl.when(j == 0)
    def _():
        acc[...] = jnp.zeros_like(acc)

    st[...] = _dot_nt(u_ref[...], h_ref[...])

    for k in range(te // N_KEYS):
        rs = slice(k * N_KEYS, (k + 1) * N_KEYS)
        for lt in range(tm // LANES):
            ls = slice(lt * LANES, (lt + 1) * LANES)
            w = jnp.zeros((N_KEYS, LANES), F32)
            for h in range(PEER_HEADS):
                n0 = n0_ref[h, k:k + 1, ls]
                e0 = e0_ref[h, k:k + 1, ls]
                w = w + jnp.where(r1_ref[h, :, ls] < n0, e1_ref[h, :, ls] * e0, 0.0)
            gs[rs, ls] = (w * jax.nn.gelu(st[rs, ls])).astype(BF16)
    acc[...] += _dot_tn(gs[...], v_ref[...])

    @pl.when(j == pl.num_programs(1) - 1)
    def _():
        x2 = x_ref[...] + g2_ref[...] * acc[...]
        if final_norm:
            x2 = x2 * lax.rsqrt(jnp.mean(x2 * x2, axis=-1, keepdims=True) + EPS) * fg_ref[...]
        o_ref[...] = x2


def _peer_call(h2, u, v, tables, x1, g2, final_g, final_norm):
    m, d = x1.shape
    ne = u.shape[0]
    tm = _row_tile(m, 512)
    te = 1024
    sub_rows = te // N_KEYS
    n0, e0, r1, e1 = tables
    small = pl.BlockSpec((PEER_HEADS, sub_rows, tm), lambda i, j: (0, j, i))
    full = pl.BlockSpec((PEER_HEADS, N_KEYS, tm), lambda i, j: (0, 0, i))
    if g2.shape[0] == 1:
        gspec = pl.BlockSpec((1, d), lambda i, j: (0, 0))
    else:
        gspec = pl.BlockSpec((tm, d), lambda i, j: (i, 0))
    return pl.pallas_call(
        functools.partial(_peer_body, te=te, tm=tm, final_norm=final_norm),
        out_shape=jax.ShapeDtypeStruct((m, d), F32),
        grid=(m // tm, ne // te),
        in_specs=[
            pl.BlockSpec((tm, d), lambda i, j: (i, 0)),
            pl.BlockSpec((te, d), lambda i, j: (j, 0)),
            pl.BlockSpec((te, d), lambda i, j: (j, 0)),
            small, small, full, full,
            pl.BlockSpec((tm, d), lambda i, j: (i, 0)),
            gspec,
            pl.BlockSpec((1, d), lambda i, j: (0, 0)),
        ],
        out_specs=pl.BlockSpec((tm, d), lambda i, j: (i, 0)),
        scratch_shapes=[pltpu.VMEM((tm, d), F32), pltpu.VMEM((te, tm), F32), pltpu.VMEM((te, tm), BF16)],
        compiler_params=_cparams("parallel", "arbitrary"),
        name="peer_dense",
    )(h2, u, v, n0, e0, r1, e1, x1, g2, final_g)


def _rope_tables(pos):
    inv = ROPE_THETA ** (-jnp.arange(0, ROT_DIM, 2, dtype=F32) / ROT_DIM)
    ang = pos.astype(F32)[:, None] * inv[None, :]
    cos, sin = jnp.cos(ang), jnp.sin(ang)
    half = ROT_DIM // 2
    pad = jnp.zeros((pos.shape[0], HEAD_DIM - ROT_DIM), F32)
    zero = jnp.zeros((pos.shape[0], half), F32)
    c = jnp.concatenate([cos, cos, pad + 1.0], axis=1)
    s_lo = jnp.concatenate([-sin, zero, pad], axis=1)
    s_hi = jnp.concatenate([zero, sin, pad], axis=1)
    return c, s_lo, s_hi


def _layer(x, pos, mods, attend, wl, final_g, final_norm):
    (norm1_g, norm2_g, w_in, ln_g, ln_b, w_s, bs_rows, w_out_a, w_out_b, w_out, w_query, sub_keys,
     expert_u, expert_v) = wl
    sh1, sc1, g1, sh2, sc2, g2 = mods
    proj = functools.partial(_proj_call, x, norm1_g, sc1, sh1, w_in)
    c0 = 0
    u = proj(c0, SGU_WIDTH, 512, _ep_gelu, name="in_u")
    c0 += SGU_WIDTH
    v_a = proj(c0, SGU_WIDTH, SGU_WIDTH, _ep_gelu_ln, col_extras=(ln_g, ln_b), name="in_v")
    c0 += SGU_WIDTH
    qk = proj(c0, 2 * ATT_WIDTH, 512, _ep_rope, row_extras=_rope_tables(pos), name="in_qk")
    c0 += 2 * ATT_WIDTH
    v = proj(c0, ATT_WIDTH, 512, _ep_none, name="in_val")
    c0 += ATT_WIDTH
    gates = proj(c0, 2 * x.shape[1], 512, _ep_sigmoid, name="in_gates")

    y_a, y_b, states = attend(u, v_a, qk, v, w_s, bs_rows)
    merged = _branch_call(y_a, y_b, w_out_a, w_out_b, gates)
    x1 = _outproj_call(merged, w_out, x, g1)

    q_heads, h2 = _proj_call(x1, norm2_g, sc2, sh2, w_query, 0, w_query.shape[1], 512, _ep_none,
                             want_h=True, split_heads=True, name="peer_query")
    tables = _select_call(q_heads, sub_keys)
    y = _peer_call(h2, expert_u, expert_v, tables, x1, g2, final_g, final_norm)
    return y, states, v_a


def _attend_prompt(u, v_a, qk, v, w_s, bs_rows):
    y_a = _sgu_call(u, v_a, w_s, bs_rows)
    t = qk.shape[0]
    outs, lses, states = [], [], []
    for g, (win, dil) in enumerate(DIL_CONFIGS):
        o, lse = _band_attn_call(qk, v, g, dil, win // dil)
        outs.append(o)
        lses.append(lse)
        keep = min(win, t)
        kcols = slice(ATT_WIDTH + g * ATT_OUT, ATT_WIDTH + (g + 1) * ATT_OUT)
        vcols = slice(g * ATT_OUT, (g + 1) * ATT_OUT)
        k_rows = qk[t - keep:, kcols].reshape(1, keep, HEADS_PER_GROUP, HEAD_DIM)
        v_rows = v[t - keep:, vcols].reshape(1, keep, HEADS_PER_GROUP, HEAD_DIM)
        states.append(jnp.stack([k_rows, v_rows], axis=2))
    return y_a, _merge_call(outs, lses), states


def _attend_sample(u, v_a, qk, v, w_s, bs_rows, *, batch, t_new, buffers, w_lanes, b_lanes):
    width = u.shape[1]
    pos_major = lambda a: jnp.transpose(a.reshape(batch, t_new, width), (1, 0, 2))
    y_a = _sgu_new_chunk_call(pos_major(u), pos_major(v_a), w_lanes, b_lanes)
    y_a = jnp.transpose(y_a, (1, 0, 2)).reshape(batch * t_new, width)
    outs, lses, states = [], [], []
    for g, (win, dil) in enumerate(DIL_CONFIGS):
        buf = buffers[g]
        win_len = buf.shape[1]
        q_g = qk[:, g * ATT_OUT:(g + 1) * ATT_OUT].reshape(batch, t_new, ATT_OUT)
        k_g = qk[:, ATT_WIDTH + g * ATT_OUT:ATT_WIDTH + (g + 1) * ATT_OUT]
        v_g = v[:, g * ATT_OUT:(g + 1) * ATT_OUT]
        new_rows = jnp.stack([k_g.reshape(batch, t_new, HEADS_PER_GROUP, HEAD_DIM),
                              v_g.reshape(batch, t_new, HEADS_PER_GROUP, HEAD_DIM)], axis=2)
        new_rows = new_rows.reshape(batch, t_new * 2 * HEADS_PER_GROUP, HEAD_DIM)
        flat = buf.reshape(batch, win_len * 2 * HEADS_PER_GROUP, HEAD_DIM)
        o, lse, new_win = _win_attn_call(q_g, new_rows, flat, dil, win // dil)
        outs.append(o.reshape(batch * t_new, ATT_OUT))
        lses.append(lse.reshape(batch * t_new, ATT_OUT))
        states.append(new_win.reshape(buf.shape))
    return y_a, _merge_call(outs, lses), states


def kernel(x_prompt, x_sample, state_win1, state_win2, state_win3, c_prompt, c_sample, norm1_g, norm2_g, w_ada, b_ada, w_in, sgu_ln_g, sgu_ln_b, w_s, b_s, w_out_a, w_out_b, w_out, w_query, sub_keys, expert_u, expert_v, final_g):
    depth = w_in.shape[0]
    bp, seq, d = x_prompt.shape
    bs, t_new, _ = x_sample.shape
    assert bp == 1
    pos_p = jnp.arange(seq, dtype=jnp.int32)
    pos_s = jnp.tile(PAST_LEN + jnp.arange(t_new, dtype=jnp.int32), bs)
    xp = x_prompt.reshape(bp * seq, d)
    xs = x_sample.reshape(bs * t_new, d)
    fg = final_g.reshape(1, d)
    n_c = bp + bs
    c_rows = -(-n_c // 16) * 16
    c_all = jnp.pad(jnp.concatenate([c_prompt, c_sample], axis=0), ((0, c_rows - n_c), (0, 0)))

    new_p = [[], [], []]
    new_s = [[], [], []]
    new_v = []
    yp = ys = None
    for l in range(depth):
        ada = _ada_call(c_all, w_ada[l].astype(BF16), b_ada[l].reshape(1, -1))
        mods_p = tuple(ada[0:bp, k * d:(k + 1) * d] for k in range(6))
        mods_s = tuple(jnp.repeat(ada[bp:n_c, k * d:(k + 1) * d], t_new, axis=0) for k in range(6))
        bs_rows = jnp.repeat(b_s[l].T, SGU_CH, axis=1)
        w_lanes = jnp.repeat(jnp.transpose(w_s[l][:, :t_new, :t_new], (1, 2, 0)), SGU_CH, axis=2)
        wl = (norm1_g[l].reshape(1, d), norm2_g[l].reshape(1, d), w_in[l].astype(BF16),
              sgu_ln_g[l].reshape(1, -1), sgu_ln_b[l].reshape(1, -1), w_s[l], bs_rows,
              w_out_a[l].astype(BF16), w_out_b[l].astype(BF16), w_out[l].astype(BF16),
              w_query[l].astype(BF16), sub_keys[l].astype(BF16), expert_u[l].astype(BF16),
              expert_v[l].astype(BF16))
        last = l == depth - 1
        yp, win_p, _ = _layer(xp, pos_p, mods_p, _attend_prompt, wl, fg, last)
        attend_s = functools.partial(_attend_sample, batch=bs, t_new=t_new,
                                     buffers=(state_win1[l], state_win2[l], state_win3[l]),
                                     w_lanes=w_lanes, b_lanes=bs_rows[:t_new])
        ys, win_s, v_rows = _layer(xs, pos_s, mods_s, attend_s, wl, fg, last)
        for g in range(N_DIL_GROUPS):
            new_p[g].append(win_p[g])
            new_s[g].append(win_s[g])
        new_v.append(v_rows.reshape(bs, t_new, SGU_WIDTH))
        xp, xs = yp, ys
    return (yp.reshape(bp, seq, d), ys.reshape(bs, t_new, d),
            jnp.stack(new_p[0]), jnp.stack(new_p[1]), jnp.stack(new_p[2]),
            jnp.stack(new_s[0]), jnp.stack(new_s[1]), jnp.stack(new_s[2]),
            jnp.stack(new_v))
```

```python
import functools

import jax
import jax.numpy as jnp
from jax import lax
from jax.experimental import pallas as pl
from jax.experimental.pallas import tpu as pltpu

F32 = jnp.float32
BF16 = jnp.bfloat16

PAST_LEN = 16384
SGU_WIDTH = 1024
SGU_GROUPS = 8
SGU_CH = SGU_WIDTH // SGU_GROUPS
CHUNK = 128
HEAD_DIM = 128
ROT_DIM = HEAD_DIM // 4
ROPE_THETA = 500000.0
DIL_CONFIGS = ((128, 1), (512, 4), (2048, 16))
N_DIL_GROUPS = 3
HEADS_PER_GROUP = 4
ATT_HEADS = N_DIL_GROUPS * HEADS_PER_GROUP
ATT_WIDTH = ATT_HEADS * HEAD_DIM
ATT_OUT = HEADS_PER_GROUP * HEAD_DIM
ATT_BLOCK = 128
PEER_HEADS = 8
N_KEYS = 128
PEER_TOPK = 16
PEER_HALF = 128
EPS = 1e-6

LANES = 128
VMEM_LIMIT_BYTES = 56 * 1024 * 1024
NEG_BIG = -1e30
NOT_RANKED = 64.0


def _cparams(*sem, flags=None):
    return pltpu.CompilerParams(dimension_semantics=sem, vmem_limit_bytes=VMEM_LIMIT_BYTES, flags=flags)


def _dot(a, b):
    return jnp.dot(a, b, preferred_element_type=F32)


def _dot_nt(a, b):
    return lax.dot_general(a, b, (((1,), (1,)), ((), ())), preferred_element_type=F32)


def _dot_tn(a, b):
    return lax.dot_general(a, b, (((0,), (0,)), ((), ())), preferred_element_type=F32)


def _row_tile(m, cap):
    t = min(m, cap)
    while m % t:
        t //= 2
    return t


def _normmod_body(x_ref, g_ref, sc_ref, sh_ref, h_ref):
    x = x_ref[...]
    y = x * lax.rsqrt(jnp.mean(x * x, axis=-1, keepdims=True) + EPS)
    h_ref[...] = ((y * g_ref[...]) * (1.0 + sc_ref[...]) + sh_ref[...]).astype(h_ref.dtype)


def _normmod_call(x, g, sc, sh):
    m, d = x.shape
    tm = _row_tile(m, 512)

    def mod_spec(a):
        if a.shape[0] == 1:
            return pl.BlockSpec((1, d), lambda i: (0, 0))
        return pl.BlockSpec((tm, d), lambda i: (i, 0))

    return pl.pallas_call(
        _normmod_body,
        out_shape=jax.ShapeDtypeStruct((m, d), BF16),
        grid=(m // tm,),
        in_specs=[pl.BlockSpec((tm, d), lambda i: (i, 0)), pl.BlockSpec((1, d), lambda i: (0, 0)),
                  mod_spec(sc), mod_spec(sh)],
        out_specs=pl.BlockSpec((tm, d), lambda i: (i, 0)),
        compiler_params=_cparams("parallel"),
        name="norm_mod",
    )(x, g, sc, sh)


def _ada_body(c_ref, w_ref, b_ref, o_ref):
    c = c_ref[...]
    a = (c * jax.nn.sigmoid(c)).astype(BF16)
    o_ref[...] = _dot(a, w_ref[...]) + b_ref[...]


def _ada_call(c_all, w_ada, b_ada):
    mc, d = c_all.shape
    n = w_ada.shape[1]
    tn = 1024
    return pl.pallas_call(
        _ada_body,
        out_shape=jax.ShapeDtypeStruct((mc, n), F32),
        grid=(n // tn,),
        in_specs=[
            pl.BlockSpec((mc, d), lambda j: (0, 0)),
            pl.BlockSpec((d, tn), lambda j: (0, j)),
            pl.BlockSpec((1, tn), lambda j: (0, j)),
        ],
        out_specs=pl.BlockSpec((mc, tn), lambda j: (0, j)),
        compiler_params=_cparams("arbitrary"),
        name="ada_proj",
    )(c_all, w_ada, b_ada)


def _ep_none(acc):
    return acc


def _ep_gelu(acc):
    return jax.nn.gelu(acc)


def _ep_sigmoid(acc):
    return jax.nn.sigmoid(acc)


def _ep_gelu_ln(acc, g, b):
    y = jax.nn.gelu(acc)
    mu = jnp.mean(y, axis=-1, keepdims=True)
    var = jnp.mean(jnp.square(y - mu), axis=-1, keepdims=True)
    return ((y - mu) * lax.rsqrt(var + EPS)) * g + b


def _ep_rope(acc, c, s_lo, s_hi):
    outs = []
    for hh in range(acc.shape[1] // LANES):
        a = acc[:, hh * LANES:(hh + 1) * LANES]
        outs.append(a * c + pltpu.roll(a, LANES - ROT_DIM // 2, 1) * s_lo + pltpu.roll(a, ROT_DIM // 2, 1) * s_hi)
    return jnp.concatenate(outs, axis=1)


def _proj_body(*refs, epilogue, n_extra, split_heads):
    h_ref, w_ref = refs[:2]
    extra = refs[2:2 + n_extra]
    out_ref = refs[2 + n_extra]
    out = epilogue(_dot(h_ref[...], w_ref[...]), *[r[...] for r in extra])
    if split_heads:
        for k in range(out.shape[1] // LANES):
            out_ref[k] = out[:, k * LANES:(k + 1) * LANES]
    else:
        out_ref[...] = out.astype(out_ref.dtype)


def _proj_call(h, w, col0, ncols, tn, epilogue, *, row_extras=(), col_extras=(), out_dtype=F32,
               split_heads=False, name="proj"):
    m, d = h.shape
    tm = _row_tile(m, 2048)
    assert col0 % tn == 0 and ncols % tn == 0
    jb = col0 // tn
    in_specs = [
        pl.BlockSpec((tm, d), lambda i, j: (i, 0)),
        pl.BlockSpec((d, tn), lambda i, j: (0, j + jb)),
    ]
    for e in row_extras:
        in_specs.append(pl.BlockSpec((tm, e.shape[1]), lambda i, j: (i, 0)))
    for e in col_extras:
        in_specs.append(pl.BlockSpec((1, tn), lambda i, j: (0, j)))
    if split_heads:
        out_shape = jax.ShapeDtypeStruct((ncols // LANES, m, LANES), out_dtype)
        out_specs = pl.BlockSpec((tn // LANES, tm, LANES), lambda i, j: (j, i, 0))
    else:
        out_shape = jax.ShapeDtypeStruct((m, ncols), out_dtype)
        out_specs = pl.BlockSpec((tm, tn), lambda i, j: (i, j))
    return pl.pallas_call(
        functools.partial(_proj_body, epilogue=epilogue, n_extra=len(row_extras) + len(col_extras),
                          split_heads=split_heads),
        out_shape=out_shape,
        grid=(m // tm, ncols // tn),
        in_specs=in_specs,
        out_specs=out_specs,
        compiler_params=_cparams("parallel", "arbitrary"),
        name=name,
    )(h, w, *row_extras, *col_extras)


def _sgu_body(u_ref, v_ref, ws_ref, bs_ref, o_ref):
    rows = u_ref.shape[0]
    causal = lax.broadcasted_iota(jnp.int32, (CHUNK, CHUNK), 0) >= lax.broadcasted_iota(jnp.int32, (CHUNK, CHUNK), 1)
    for g in range(SGU_GROUPS):
        ws = jnp.where(causal, ws_ref[g], 0.0).astype(BF16)
        cols = slice(g * LANES, (g + 1) * LANES)
        for c in range(rows // CHUNK):
            rs = slice(c * CHUNK, (c + 1) * CHUNK)
            mixed = _dot(ws, v_ref[rs, cols].astype(BF16)) + bs_ref[:, cols]
            o_ref[rs, cols] = (u_ref[rs, cols] * mixed).astype(o_ref.dtype)


def _sgu_call(u, v, w_s, bs_rows):
    m, w = u.shape
    rb = _row_tile(m, 512)
    return pl.pallas_call(
        _sgu_body,
        out_shape=jax.ShapeDtypeStruct((m, w), BF16),
        grid=(m // rb,),
        in_specs=[
            pl.BlockSpec((rb, w), lambda i: (i, 0)),
            pl.BlockSpec((rb, w), lambda i: (i, 0)),
            pl.BlockSpec(w_s.shape, lambda i: (0, 0, 0)),
            pl.BlockSpec(bs_rows.shape, lambda i: (0, 0)),
        ],
        out_specs=pl.BlockSpec((rb, w), lambda i: (i, 0)),
        compiler_params=_cparams("parallel"),
        name="sgu_mix",
    )(u, v, w_s, bs_rows)


def _sgu_new_chunk_body(u_ref, v_ref, wl_ref, bl_ref, o_ref):
    t_new = u_ref.shape[0]
    for i in range(t_new):
        mixed = bl_ref[i:i + 1, :]
        for j in range(i + 1):
            mixed = mixed + wl_ref[i, j:j + 1, :] * v_ref[j]
        o_ref[i] = (u_ref[i] * mixed).astype(o_ref.dtype)


def _sgu_new_chunk_call(u, v, w_lanes, b_lanes):
    return pl.pallas_call(
        _sgu_new_chunk_body,
        out_shape=jax.ShapeDtypeStruct(u.shape, BF16),
        name="sgu_mix_new_chunk",
    )(u, v, w_lanes, b_lanes)


def _band_attn_body(q_ref, kc_ref, kp_ref, vc_ref, vp_ref, o_ref, lse_ref, kx, vx, *, dil, reach, rb):
    span = dil * ATT_BLOCK
    n = pl.program_id(0)
    kx[0:span, :] = kp_ref[...]
    kx[span:span + rb, :] = kc_ref[...]
    vx[0:span, :] = vp_ref[...]
    vx[span:span + rb, :] = vc_ref[...]
    qi = lax.broadcasted_iota(jnp.int32, (ATT_BLOCK, ATT_BLOCK), 0)
    kj = lax.broadcasted_iota(jnp.int32, (ATT_BLOCK, ATT_BLOCK), 1)
    dist_c = qi - kj
    dist_p = ATT_BLOCK + qi - kj
    mask_c = (dist_c >= 0) & (dist_c <= reach)
    mask_p0 = (dist_p >= 0) & (dist_p <= reach)
    scale = HEAD_DIM ** -0.5

    def rows(start):
        if dil == 1:
            return pl.ds(start, ATT_BLOCK)
        return pl.ds(start, ATT_BLOCK, stride=dil)

    def step(it, carry):
        a = it // dil
        r = it % dil
        base = a * span + r
        q = q_ref[rows(base), :].astype(BF16)
        kc = kx[rows(span + base), :].astype(BF16)
        kp = kx[rows(base), :].astype(BF16)
        vc = vx[rows(span + base), :].astype(BF16)
        vp = vx[rows(base), :].astype(BF16)
        has_prev = (n * rb + a * span) > 0
        s_c = jnp.where(mask_c, _dot_nt(q, kc) * scale, NEG_BIG)
        s_p = jnp.where(mask_p0, _dot_nt(q, kp) * scale, NEG_BIG)
        s_p = jnp.where(has_prev, s_p, NEG_BIG)
        m = jnp.maximum(jnp.max(s_c, axis=1, keepdims=True), jnp.max(s_p, axis=1, keepdims=True))
        p_c = jnp.exp(s_c - m)
        p_p = jnp.exp(s_p - m)
        l = jnp.sum(p_c, axis=1, keepdims=True) + jnp.sum(p_p, axis=1, keepdims=True)
        o = (_dot(p_c.astype(BF16), vc) + _dot(p_p.astype(BF16), vp)) / l
        o_ref[rows(base), :] = o
        lse_ref[rows(base), :] = jnp.broadcast_to(m + jnp.log(l), (ATT_BLOCK, HEAD_DIM))
        return carry

    lax.fori_loop(0, rb // ATT_BLOCK, step, 0, unroll=4)


def _band_attn_call(qk, v, group, dil, reach):
    t = qk.shape[0]
    span = dil * ATT_BLOCK
    rb = max(span, _row_tile(t, 2048))
    assert t % rb == 0 and rb % span == 0
    per = rb // span
    qcol = group * HEADS_PER_GROUP
    kcol = ATT_HEADS + group * HEADS_PER_GROUP
    cur = lambda c0: pl.BlockSpec((rb, HEAD_DIM), lambda n, h: (n, c0 + h))
    prev = lambda c0: pl.BlockSpec((span, HEAD_DIM), lambda n, h: (jnp.maximum(n * per - 1, 0), c0 + h))
    out_spec = pl.BlockSpec((rb, HEAD_DIM), lambda n, h: (n, h))
    return pl.pallas_call(
        functools.partial(_band_attn_body, dil=dil, reach=reach, rb=rb),
        out_shape=[jax.ShapeDtypeStruct((t, ATT_OUT), F32)] * 2,
        grid=(t // rb, HEADS_PER_GROUP),
        in_specs=[cur(qcol), cur(kcol), prev(kcol), cur(qcol), prev(qcol)],
        out_specs=[out_spec, out_spec],
        scratch_shapes=[pltpu.VMEM((span + rb, HEAD_DIM), F32)] * 2,
        compiler_params=_cparams("parallel", "parallel"),
        name=f"band_attn_d{dil}",
    )(qk, qk, qk, v, v)


def _win_attn_body(q_ref, new_ref, st_ref, o_ref, lse_ref, win_ref, *, dil, reach, t_new, win_len):
    rows_per_pos = 2 * HEADS_PER_GROUP
    shift = t_new * rows_per_pos
    total = win_len * rows_per_pos
    win_ref[0, 0:total - shift, :] = st_ref[0, shift:total, :]
    win_ref[0, total - shift:total, :] = new_ref[0]
    scale = HEAD_DIM ** -0.5
    m_idx = lax.broadcasted_iota(jnp.int32, (ATT_BLOCK, 1), 0)
    for t in range(t_new):
        res = t % dil
        first = t // dil
        for h in range(HEADS_PER_GROUP):
            q = q_ref[0, t:t + 1, h * HEAD_DIM:(h + 1) * HEAD_DIM]
            kb = st_ref[0, pl.ds(res * rows_per_pos + h, ATT_BLOCK, stride=rows_per_pos * dil), :]
            vb = st_ref[0, pl.ds(res * rows_per_pos + HEADS_PER_GROUP + h, ATT_BLOCK, stride=rows_per_pos * dil), :]
            s_b = jnp.sum(kb * q, axis=1, keepdims=True) * scale
            s_b = jnp.where(m_idx >= first, s_b, NEG_BIG)
            new_keys = [tp for tp in range(t + 1) if (t - tp) % dil == 0 and (t - tp) // dil <= reach]
            s_n = []
            for tp in new_keys:
                kn = new_ref[0, tp * rows_per_pos + h:tp * rows_per_pos + h + 1, :]
                s_n.append(jnp.sum(kn * q, axis=1, keepdims=True) * scale)
            mx = jnp.max(s_b, axis=0, keepdims=True)
            for s in s_n:
                mx = jnp.maximum(mx, s)
            p_b = jnp.exp(s_b - mx)
            l = jnp.sum(p_b, axis=0, keepdims=True)
            acc = jnp.sum(p_b * vb, axis=0, keepdims=True)
            for tp, s in zip(new_keys, s_n):
                p = jnp.exp(s - mx)
                l = l + p
                r0 = tp * rows_per_pos + HEADS_PER_GROUP + h
                acc = acc + p * new_ref[0, r0:r0 + 1, :]
            o_ref[0, t:t + 1, h * HEAD_DIM:(h + 1) * HEAD_DIM] = acc / l
            lse_ref[0, t:t + 1, h * HEAD_DIM:(h + 1) * HEAD_DIM] = jnp.broadcast_to(mx + jnp.log(l), (1, HEAD_DIM))


def _win_attn_call(q_g, new_rows, state, dil, reach):
    b, t_new, _ = q_g.shape
    flat = state.shape[1]
    win_len = flat // (2 * HEADS_PER_GROUP)
    assert win_len == reach * dil and reach == ATT_BLOCK and t_new * 2 * HEADS_PER_GROUP % 8 == 0
    qspec = pl.BlockSpec((1, t_new, ATT_OUT), lambda i: (i, 0, 0))
    return pl.pallas_call(
        functools.partial(_win_attn_body, dil=dil, reach=reach, t_new=t_new, win_len=win_len),
        out_shape=[jax.ShapeDtypeStruct(q_g.shape, F32), jax.ShapeDtypeStruct(q_g.shape, F32),
                   jax.ShapeDtypeStruct(state.shape, F32)],
        grid=(b,),
        in_specs=[qspec,
                  pl.BlockSpec((1,) + new_rows.shape[1:], lambda i: (i, 0, 0)),
                  pl.BlockSpec((1, flat, HEAD_DIM), lambda i: (i, 0, 0))],
        out_specs=[qspec, qspec, pl.BlockSpec((1, flat, HEAD_DIM), lambda i: (i, 0, 0))],
        compiler_params=_cparams("parallel"),
        name=f"win_attn_d{dil}",
    )(q_g, new_rows, state)


def _merge_body(o1, o2, o3, l1, l2, l3, y_ref):
    a, b, c = l1[...], l2[...], l3[...]
    mx = jnp.maximum(jnp.maximum(a, b), c)
    ea, eb, ec = jnp.exp(a - mx), jnp.exp(b - mx), jnp.exp(c - mx)
    y = (ea * o1[...] + eb * o2[...] + ec * o3[...]) / (ea + eb + ec)
    y_ref[...] = y.astype(y_ref.dtype)


def _merge_call(outs, lses):
    m, w = outs[0].shape
    tm = _row_tile(m, 1024)
    spec = pl.BlockSpec((tm, w), lambda i: (i, 0))
    return pl.pallas_call(
        _merge_body,
        out_shape=jax.ShapeDtypeStruct((m, w), BF16),
        grid=(m // tm,),
        in_specs=[spec] * 6,
        out_specs=spec,
        compiler_params=_cparams("parallel"),
        name="merge_groups",
    )(*outs, *lses)


def _branch_body(ya_ref, yb_ref, wa_ref, wb_ref, ga_ref, gb_ref, o_ref):
    o = ga_ref[...] * _dot(ya_ref[...], wa_ref[...]) + gb_ref[...] * _dot(yb_ref[...], wb_ref[...])
    o_ref[...] = o.astype(o_ref.dtype)


def _branch_call(ya, yb, w_out_a, w_out_b, gates):
    m = ya.shape[0]
    d = w_out_a.shape[1]
    tm = _row_tile(m, 1024)
    tn = 512
    nj = d // tn
    return pl.pallas_call(
        _branch_body,
        out_shape=jax.ShapeDtypeStruct((m, d), BF16),
        grid=(m // tm, nj),
        in_specs=[
            pl.BlockSpec((tm, ya.shape[1]), lambda i, j: (i, 0)),
            pl.BlockSpec((tm, yb.shape[1]), lambda i, j: (i, 0)),
            pl.BlockSpec((ya.shape[1], tn), lambda i, j: (0, j)),
            pl.BlockSpec((yb.shape[1], tn), lambda i, j: (0, j)),
            pl.BlockSpec((tm, tn), lambda i, j: (i, j)),
            pl.BlockSpec((tm, tn), lambda i, j: (i, j + nj)),
        ],
        out_specs=pl.BlockSpec((tm, tn), lambda i, j: (i, j)),
        compiler_params=_cparams("parallel", "arbitrary"),
        name="branch_proj",
    )(ya, yb, w_out_a, w_out_b, gates, gates)


def _outproj_body(mg_ref, w_ref, x_ref, g_ref, o_ref):
    o_ref[...] = x_ref[...] + g_ref[...] * _dot(mg_ref[...], w_ref[...])


def _outproj_call(merged, w_out, x, g1):
    m, d = x.shape
    tm = _row_tile(m, 1024)
    tn = 512
    if g1.shape[0] == 1:
        gspec = pl.BlockSpec((1, tn), lambda i, j: (0, j))
    else:
        gspec = pl.BlockSpec((tm, tn), lambda i, j: (i, j))
    return pl.pallas_call(
        _outproj_body,
        out_shape=jax.ShapeDtypeStruct((m, d), F32),
        grid=(m // tm, d // tn),
        in_specs=[
            pl.BlockSpec((tm, d), lambda i, j: (i, 0)),
            pl.BlockSpec((d, tn), lambda i, j: (0, j)),
            pl.BlockSpec((tm, tn), lambda i, j: (i, j)),
            gspec,
        ],
        out_specs=pl.BlockSpec((tm, tn), lambda i, j: (i, j)),
        compiler_params=_cparams("parallel", "arbitrary"),
        name="out_proj",
    )(merged, w_out, x, g1)


def _take_topk(s, order, break_ties):
    vals, hots = [], []
    for _ in range(PEER_TOPK):
        m = jnp.max(s, axis=0, keepdims=True)
        hot = s == m
        if break_ties:
            first = jnp.min(jnp.where(hot, order, float(1 << 20)), axis=0, keepdims=True)
            hot = order == first
        s = jnp.where(hot, -jnp.inf, s)
        vals.append(m)
        hots.append(hot)
    return vals, hots


def _select_tables(q_ref, sk_ref, break_ties):
    tq = q_ref.shape[1]
    key_idx = lax.broadcasted_iota(jnp.int32, (N_KEYS, tq), 0).astype(F32)
    scores, tops, ranks = [], [], []
    n_taken = jnp.zeros((1, tq), F32)
    for c in range(2):
        s = _dot_nt(sk_ref[0, c], q_ref[c].astype(BF16))
        vals, hots = _take_topk(s, key_idx, break_ties)
        rank = jnp.full((N_KEYS, tq), NOT_RANKED, F32)
        for r, hot in enumerate(hots):
            rank = jnp.where(hot, float(r), rank)
        n_taken = n_taken + jnp.sum(jnp.where(rank < NOT_RANKED, 1.0, 0.0), axis=0, keepdims=True)
        scores.append(s)
        tops.append(vals)
        ranks.append(rank)

    top1 = jnp.concatenate(tops[1], axis=0)
    parts = [tops[0][0] + top1] + [tops[0][a] + top1[0:8] for a in range(1, PEER_TOPK)]
    cand = jnp.concatenate(parts, axis=0)
    n_rows = cand.shape[0]
    row = lax.broadcasted_iota(jnp.int32, (n_rows, tq), 0)
    a_idx = jnp.where(row < PEER_TOPK, 0, ((row - PEER_TOPK) >> 3) + 1)
    b_idx = jnp.where(row < PEER_TOPK, row, (row - PEER_TOPK) & 7)
    cand = jnp.where((a_idx + 1) * (b_idx + 1) <= PEER_TOPK, cand, -jnp.inf)
    vals, hots = _take_topk(cand, (a_idx * PEER_TOPK + b_idx).astype(F32), break_ties)
    z = jnp.zeros_like(vals[0])
    taken = jnp.zeros((n_rows, tq), F32)
    for m, hot in zip(vals, hots):
        z = z + jnp.exp(m - vals[0])
        taken = jnp.where(hot, 1.0, taken)
    n_taken = n_taken + jnp.sum(taken, axis=0, keepdims=True)
    counts = [jnp.sum(taken[0:PEER_TOPK], axis=0, keepdims=True)]
    for a in range(1, PEER_TOPK):
        lo = PEER_TOPK + 8 * (a - 1)
        counts.append(jnp.sum(taken[lo:lo + 8], axis=0, keepdims=True))
    n0 = jnp.zeros((N_KEYS, tq), F32)
    for a in range(PEER_TOPK):
        n0 = n0 + jnp.where(ranks[0] == float(a), counts[a], 0.0)
    e0 = jnp.exp(scores[0] - tops[0][0]) / z
    e1 = jnp.exp(scores[1] - tops[1][0])
    return (n0, e0, ranks[1], e1), n_taken


def _select_body(q_ref, sk_ref, n0_ref, e0_ref, r1_ref, e1_ref):
    def run(break_ties):
        tables, n_taken = _select_tables(q_ref, sk_ref, break_ties)
        for ref, t in zip((n0_ref, e0_ref, r1_ref, e1_ref), tables):
            for s in range(t.shape[1] // LANES):
                ref[0, s] = t[:, s * LANES:(s + 1) * LANES].astype(ref.dtype)
        return n_taken

    n_taken = run(False)
    tied = jnp.max(jnp.abs(n_taken - 3.0 * PEER_TOPK)) > 0.0

    @pl.when(tied)
    def _():
        run(True)


def _peer_chunk(m):
    return _row_tile(m, 256)


def _select_call(q_heads, sub_keys):
    _, m, _ = q_heads.shape
    tq = _peer_chunk(m)
    tspec = pl.BlockSpec((1, tq // LANES, N_KEYS, LANES), lambda i, h: (h, i, 0, 0))
    shape = (PEER_HEADS, m // LANES, N_KEYS, LANES)
    return pl.pallas_call(
        _select_body,
        out_shape=[jax.ShapeDtypeStruct(shape, F32)] * 4,
        grid=(m // tq, PEER_HEADS),
        in_specs=[
            pl.BlockSpec((2, tq, PEER_HALF), lambda i, h: (h, i, 0)),
            pl.BlockSpec((1, 2, N_KEYS, PEER_HALF), lambda i, h: (h, 0, 0, 0)),
        ],
        out_specs=[tspec] * 4,
        compiler_params=_cparams("parallel", "parallel"),
        name="peer_select",
    )(q_heads, sub_keys)


def _peer_body(h_ref, u_ref, v_ref, n0_ref, e0_ref, r1_ref, e1_ref, x_ref, g2_ref, fg_ref, o_ref,
               acc, act, gs, *, te, tm, final_norm):
    j = pl.program_id(1)
    slabs = te // N_KEYS
    group = 2
    pack = 16
    subs = N_KEYS // pack

    @pl.when(j == 0)
    def _():
        acc[...] = jnp.zeros_like(acc)

    zero = jnp.zeros((), BF16)
    halves = 2
    half = te // halves

    def weights(k_lo, k_hi):
        for lt in range(tm // LANES):
            ls = slice(lt * LANES, (lt + 1) * LANES)
            for k0 in range(k_lo, k_hi, group):
                w = [[None] * subs for _ in range(group)]
                for h in range(PEER_HEADS):
                    r1 = [r1_ref[h, lt, s * pack:(s + 1) * pack, :].astype(BF16) for s in range(subs)]
                    e1 = [e1_ref[h, lt, s * pack:(s + 1) * pack, :].astype(BF16) for s in range(subs)]
                    for k in range(group):
                        n0 = jnp.broadcast_to(n0_ref[h, lt, k0 + k:k0 + k + 1, :], (pack, LANES)).astype(BF16)
                        e0 = jnp.broadcast_to(e0_ref[h, lt, k0 + k:k0 + k + 1, :], (pack, LANES)).astype(BF16)
                        for s in range(subs):
                            term = jnp.where(r1[s] < n0, e1[s] * e0, zero)
                            w[k][s] = term if h == 0 else w[k][s] + term
                for k in range(group):
                    for s in range(subs):
                        r0 = (k0 + k) * N_KEYS + s * pack
                        gs[r0:r0 + pack, ls] = w[k][s] * jax.nn.gelu(act[r0:r0 + pack, ls].astype(BF16))

    for p in range(halves):
        rows = slice(p * half, (p + 1) * half)
        act[rows, :] = _dot_nt(u_ref[rows, :], h_ref[...])
    for p in range(halves):
        rows = slice(p * half, (p + 1) * half)
        weights(p * half // N_KEYS, (p + 1) * half // N_KEYS)
        acc[...] += _dot_tn(gs[rows, :], v_ref[rows, :])

    @pl.when(j == pl.num_programs(1) - 1)
    def _():
        x2 = x_ref[...] + g2_ref[...] * acc[...]
        if final_norm:
            x2 = x2 * lax.rsqrt(jnp.mean(x2 * x2, axis=-1, keepdims=True) + EPS) * fg_ref[...]
        o_ref[...] = x2


def _peer_call(h2, u, v, tables, x1, g2, final_g, final_norm):
    m, d = x1.shape
    ne = u.shape[0]
    tm = _row_tile(m, 512)
    te = 1024
    n0, e0, r1, e1 = tables
    small = pl.BlockSpec((PEER_HEADS, tm // LANES, te // N_KEYS, LANES), lambda i, j: (0, i, j, 0))
    full = pl.BlockSpec((PEER_HEADS, tm // LANES, N_KEYS, LANES), lambda i, j: (0, i, 0, 0))
    if g2.shape[0] == 1:
        gspec = pl.BlockSpec((1, d), lambda i, j: (0, 0))
    else:
        gspec = pl.BlockSpec((tm, d), lambda i, j: (i, 0))
    return pl.pallas_call(
        functools.partial(_peer_body, te=te, tm=tm, final_norm=final_norm),
        out_shape=jax.ShapeDtypeStruct((m, d), F32),
        grid=(m // tm, ne // te),
        in_specs=[
            pl.BlockSpec((tm, d), lambda i, j: (i, 0)),
            pl.BlockSpec((te, d), lambda i, j: (j, 0)),
            pl.BlockSpec((te, d), lambda i, j: (j, 0)),
            small, small, full, full,
            pl.BlockSpec((tm, d), lambda i, j: (i, 0), pipeline_mode=pl.Buffered(1)),
            gspec,
            pl.BlockSpec((1, d), lambda i, j: (0, 0)),
        ],
        out_specs=pl.BlockSpec((tm, d), lambda i, j: (i, 0)),
        scratch_shapes=[pltpu.VMEM((tm, d), F32), pltpu.VMEM((te, tm), F32), pltpu.VMEM((te, tm), BF16)],
        compiler_params=_cparams("parallel", "arbitrary"),
        name="peer_dense",
    )(h2, u, v, n0, e0, r1, e1, x1, g2, final_g)


def _rope_tables(pos):
    inv = ROPE_THETA ** (-jnp.arange(0, ROT_DIM, 2, dtype=F32) / ROT_DIM)
    ang = pos.astype(F32)[:, None] * inv[None, :]
    cos, sin = jnp.cos(ang), jnp.sin(ang)
    half = ROT_DIM // 2
    pad = jnp.zeros((pos.shape[0], HEAD_DIM - ROT_DIM), F32)
    zero = jnp.zeros((pos.shape[0], half), F32)
    c = jnp.concatenate([cos, cos, pad + 1.0], axis=1)
    s_lo = jnp.concatenate([-sin, zero, pad], axis=1)
    s_hi = jnp.concatenate([zero, sin, pad], axis=1)
    return c, s_lo, s_hi


def _layer(x, pos, mods, attend, wl, final_g, final_norm):
    (norm1_g, norm2_g, w_in, ln_g, ln_b, w_s, bs_rows, w_out_a, w_out_b, w_out, w_query, sub_keys,
     expert_u, expert_v) = wl
    sh1, sc1, g1, sh2, sc2, g2 = mods
    proj = functools.partial(_proj_call, _normmod_call(x, norm1_g, sc1, sh1), w_in)
    c0 = 0
    u = proj(c0, SGU_WIDTH, 512, _ep_gelu, out_dtype=BF16, name="in_u")
    c0 += SGU_WIDTH
    v_a = proj(c0, SGU_WIDTH, SGU_WIDTH, _ep_gelu_ln, col_extras=(ln_g, ln_b), name="in_v")
    c0 += SGU_WIDTH
    qk = proj(c0, 2 * ATT_WIDTH, 512, _ep_rope, row_extras=_rope_tables(pos), name="in_qk")
    c0 += 2 * ATT_WIDTH
    v = proj(c0, ATT_WIDTH, 512, _ep_none, name="in_val")
    c0 += ATT_WIDTH
    gates = proj(c0, 2 * x.shape[1], 512, _ep_sigmoid, out_dtype=BF16, name="in_gates")

    y_a, y_b, states = attend(u, v_a, qk, v, w_s, bs_rows)
    merged = _branch_call(y_a, y_b, w_out_a, w_out_b, gates)
    x1 = _outproj_call(merged, w_out, x, g1)

    h2 = _normmod_call(x1, norm2_g, sc2, sh2)
    q_heads = _proj_call(h2, w_query, 0, w_query.shape[1], 512, _ep_none, split_heads=True, name="peer_query")
    tables = _select_call(q_heads, sub_keys)
    y = _peer_call(h2, expert_u, expert_v, tables, x1, g2, final_g, final_norm)
    return y, states, v_a


def _attend_prompt(u, v_a, qk, v, w_s, bs_rows):
    y_a = _sgu_call(u, v_a, w_s, bs_rows)
    t = qk.shape[0]
    outs, lses, states = [], [], []
    for g, (win, dil) in enumerate(DIL_CONFIGS):
        o, lse = _band_attn_call(qk, v, g, dil, win // dil)
        outs.append(o)
        lses.append(lse)
        keep = min(win, t)
        kcols = slice(ATT_WIDTH + g * ATT_OUT, ATT_WIDTH + (g + 1) * ATT_OUT)
        vcols = slice(g * ATT_OUT, (g + 1) * ATT_OUT)
        k_rows = qk[t - keep:, kcols].reshape(1, keep, HEADS_PER_GROUP, HEAD_DIM)
        v_rows = v[t - keep:, vcols].reshape(1, keep, HEADS_PER_GROUP, HEAD_DIM)
        states.append(jnp.stack([k_rows, v_rows], axis=2))
    return y_a, _merge_call(outs, lses), states


def _attend_sample(u, v_a, qk, v, w_s, bs_rows, *, batch, t_new, buffers, w_lanes, b_lanes):
    width = u.shape[1]
    pos_major = lambda a: jnp.transpose(a.reshape(batch, t_new, width), (1, 0, 2))
    y_a = _sgu_new_chunk_call(pos_major(u), pos_major(v_a), w_lanes, b_lanes)
    y_a = jnp.transpose(y_a, (1, 0, 2)).reshape(batch * t_new, width)
    outs, lses, states = [], [], []
    for g, (win, dil) in enumerate(DIL_CONFIGS):
        buf = buffers[g]
        win_len = buf.shape[1]
        q_g = qk[:, g * ATT_OUT:(g + 1) * ATT_OUT].reshape(batch, t_new, ATT_OUT)
        k_g = qk[:, ATT_WIDTH + g * ATT_OUT:ATT_WIDTH + (g + 1) * ATT_OUT]
        v_g = v[:, g * ATT_OUT:(g + 1) * ATT_OUT]
        new_rows = jnp.stack([k_g.reshape(batch, t_new, HEADS_PER_GROUP, HEAD_DIM),
                              v_g.reshape(batch, t_new, HEADS_PER_GROUP, HEAD_DIM)], axis=2)
        new_rows = new_rows.reshape(batch, t_new * 2 * HEADS_PER_GROUP, HEAD_DIM)
        flat = buf.reshape(batch, win_len * 2 * HEADS_PER_GROUP, HEAD_DIM)
        o, lse, new_win = _win_attn_call(q_g, new_rows, flat, dil, win // dil)
        outs.append(o.reshape(batch * t_new, ATT_OUT))
        lses.append(lse.reshape(batch * t_new, ATT_OUT))
        states.append(new_win.reshape(buf.shape))
    return y_a, _merge_call(outs, lses), states


def kernel(x_prompt, x_sample, state_win1, state_win2, state_win3, c_prompt, c_sample, norm1_g, norm2_g, w_ada, b_ada, w_in, sgu_ln_g, sgu_ln_b, w_s, b_s, w_out_a, w_out_b, w_out, w_query, sub_keys, expert_u, expert_v, final_g):
    depth = w_in.shape[0]
    bp, seq, d = x_prompt.shape
    bs, t_new, _ = x_sample.shape
    assert bp == 1
    pos_p = jnp.arange(seq, dtype=jnp.int32)
    pos_s = jnp.tile(PAST_LEN + jnp.arange(t_new, dtype=jnp.int32), bs)
    xp = x_prompt.reshape(bp * seq, d)
    xs = x_sample.reshape(bs * t_new, d)
    fg = final_g.reshape(1, d)
    n_c = bp + bs
    c_rows = -(-n_c // 16) * 16
    c_all = jnp.pad(jnp.concatenate([c_prompt, c_sample], axis=0), ((0, c_rows - n_c), (0, 0)))

    new_p = [[], [], []]
    new_s = [[], [], []]
    new_v = []
    yp = ys = None
    for l in range(depth):
        ada = _ada_call(c_all, w_ada[l].astype(BF16), b_ada[l].reshape(1, -1))
        mods_p = tuple(ada[0:bp, k * d:(k + 1) * d] for k in range(6))
        mods_s = tuple(jnp.repeat(ada[bp:n_c, k * d:(k + 1) * d], t_new, axis=0) for k in range(6))
        bs_rows = jnp.repeat(b_s[l].T, SGU_CH, axis=1)
        w_lanes = jnp.repeat(jnp.transpose(w_s[l][:, :t_new, :t_new], (1, 2, 0)), SGU_CH, axis=2)
        wl = (norm1_g[l].reshape(1, d), norm2_g[l].reshape(1, d), w_in[l].astype(BF16),
              sgu_ln_g[l].reshape(1, -1), sgu_ln_b[l].reshape(1, -1), w_s[l], bs_rows,
              w_out_a[l].astype(BF16), w_out_b[l].astype(BF16), w_out[l].astype(BF16),
              w_query[l].astype(BF16), sub_keys[l].astype(BF16), expert_u[l].astype(BF16),
              expert_v[l].astype(BF16))
        last = l == depth - 1
        yp, win_p, _ = _layer(xp, pos_p, mods_p, _attend_prompt, wl, fg, last)
        attend_s = functools.partial(_attend_sample, batch=bs, t_new=t_new,
                                     buffers=(state_win1[l], state_win2[l], state_win3[l]),
                                     w_lanes=w_lanes, b_lanes=bs_rows[:t_new])
        ys, win_s, v_rows = _layer(xs, pos_s, mods_s, attend_s, wl, fg, last)
        for g in range(N_DIL_GROUPS):
            new_p[g].append(win_p[g])
            new_s[g].append(win_s[g])
        new_v.append(v_rows.reshape(bs, t_new, SGU_WIDTH))
        xp, xs = yp, ys
    return (yp.reshape(bp, seq, d), ys.reshape(bs, t_new, d),
            jnp.stack(new_p[0]), jnp.stack(new_p[1]), jnp.stack(new_p[2]),
            jnp.stack(new_s[0]), jnp.stack(new_s[1]), jnp.stack(new_s[2]),
            jnp.stack(new_v))
```
